```python
import math
import jax, jax.numpy as jnp
from jax import lax
import numpy as np

D_MODEL = 1024
BATCH = 4
SEQ = 4096
DEPTH = 1

N_Q_HEADS = 8
N_KV_HEADS = 2
HEAD_DIM = 64
ATTN_WIDTH = N_Q_HEADS * HEAD_DIM
KV_WIDTH = N_KV_HEADS * HEAD_DIM
WINDOW = 128
BLOCK = 128
ROPE_DIM = HEAD_DIM // 4
ROPE_THETA = 500000.0
SSM_WIDTH = D_MODEL // 2
SSM_GROUP = 16
N_SSM_GROUPS = SSM_WIDTH // SSM_GROUP
SSM_STATE = 64
N_DIRS = 2
DT_MIN = 1e-3
DT_MAX = 1e-1
MIX_WIDTH = ATTN_WIDTH + SSM_WIDTH
IN_WIDTH = ATTN_WIDTH + 2 * KV_WIDTH + SSM_WIDTH
D_FF = 2816
CONV_WIDTH = 3
EPS = 1e-6

kernel_name = "hymba_s5_swa_convffn_encoder"

F32 = jnp.float32


def rms_norm(x, g):
    xf = x.astype(F32)
    y = xf * lax.rsqrt(jnp.mean(xf * xf, axis=-1, keepdims=True) + EPS)
    return (y * g.astype(F32)).astype(x.dtype)


def partial_rope(t, pos):
    half = ROPE_DIM // 2
    inv_freq = jnp.power(ROPE_THETA, -jnp.arange(half, dtype=F32) / half)
    ang = pos.astype(F32)[:, None] * inv_freq[None, :]
    cos = jnp.cos(ang)[None, :, None, :]
    sin = jnp.sin(ang)[None, :, None, :]
    tf = t.astype(F32)
    t1 = tf[..., :half]
    t2 = tf[..., half:ROPE_DIM]
    rest = tf[..., ROPE_DIM:]
    out = jnp.concatenate([t1 * cos - t2 * sin, t2 * cos + t1 * sin, rest], axis=-1)
    return out.astype(t.dtype)


def window_attention(q, k, v, sink):
    b, l = q.shape[0], q.shape[1]
    nb = l // BLOCK
    grp = N_Q_HEADS // N_KV_HEADS
    qb = q.astype(F32).reshape(b, nb, BLOCK, N_KV_HEADS, grp, HEAD_DIM)

    def band(t):
        tp = jnp.pad(t.astype(F32), ((0, 0), (BLOCK, BLOCK), (0, 0), (0, 0)))
        tb = tp.reshape(b, nb + 2, BLOCK, N_KV_HEADS, HEAD_DIM)
        return jnp.concatenate([tb[:, :-2], tb[:, 1:-1], tb[:, 2:]], axis=2)

    kw = band(k)
    vw = band(v)
    s = jnp.einsum('bnqhgd,bnkhd->bnhgqk', qb, kw) * (HEAD_DIM ** -0.5)
    qpos = jnp.arange(nb)[:, None] * BLOCK + jnp.arange(BLOCK)[None, :]
    kpos = (jnp.arange(nb)[:, None] - 1) * BLOCK + jnp.arange(3 * BLOCK)[None, :]
    rel = kpos[:, None, :] - qpos[:, :, None]
    valid = (jnp.abs(rel) <= WINDOW) & (kpos[:, None, :] >= 0) & (kpos[:, None, :] < l)
    s = jnp.where(valid[None, :, None, None], s, -jnp.inf)
    sink_l = sink.astype(F32).reshape(N_KV_HEADS, grp)[None, None, :, :, None, None]
    m = jnp.maximum(jnp.max(s, axis=-1, keepdims=True), sink_l)
    p = jnp.exp(s - m)
    denom = jnp.sum(p, axis=-1, keepdims=True) + jnp.exp(sink_l - m)
    o = jnp.einsum('bnhgqk,bnkhd->bnqhgd', p / denom, vw)
    return o.reshape(b, l, ATTN_WIDTH)


def _scan_op(e1, e2):
    a1, x1 = e1
    a2, x2 = e2
    return a1 * a2, a2 * x1 + x2


def s5_bidirectional(u, a_re, a_im, log_step, b_re, b_im, c_re, c_im, d_skip):
    bsz, l = u.shape[0], u.shape[1]
    ug = u.astype(F32).reshape(bsz, l, N_SSM_GROUPS, SSM_GROUP)
    lam = lax.complex(a_re.astype(F32), a_im.astype(F32))
    step = jnp.exp(log_step.astype(F32))[..., None]
    lam_bar = jnp.exp(lam * step)
    bmat = lax.complex(b_re.astype(F32), b_im.astype(F32))
    b_bar = ((lam_bar - 1.0) / lam)[..., None] * bmat
    cmat = lax.complex(c_re.astype(F32), c_im.astype(F32))
    y = d_skip.astype(F32)[None, None] * ug
    for direction, rev in ((0, False), (1, True)):
        bu = jnp.einsum('blgc,gpc->blgp', ug, b_bar[direction])
        a = jnp.broadcast_to(lam_bar[direction][None, None], bu.shape)
        _, states = lax.associative_scan(_scan_op, (a, bu), reverse=rev, axis=1)
        y = y + jnp.real(jnp.einsum('gcp,blgp->blgc', cmat[direction], states))
    return y.reshape(bsz, l, SSM_WIDTH)


def depthwise_conv(t, w, bias):
    c = t.shape[-1]
    pad = CONV_WIDTH // 2
    out = lax.conv_general_dilated(t, w[:, None, :].astype(t.dtype), window_strides=(1,),
                                   padding=((pad, pad),), dimension_numbers=('NWC', 'WIO', 'NWC'),
                                   feature_group_count=c)
    return out + bias.astype(out.dtype)


def setup_inputs(seed: int = 0) -> dict:
    key = jax.random.key(seed)
    ks = jax.random.split(key, 24)
    G, P, C = N_SSM_GROUPS, SSM_STATE, SSM_GROUP
    nrm = lambda k, shape: jax.random.normal(k, shape, dtype=F32)
    x = nrm(ks[0], (BATCH, SEQ, D_MODEL))
    norm_mix_g = 1.0 + 0.02 * nrm(ks[1], (DEPTH, D_MODEL))
    w_in = nrm(ks[2], (DEPTH, D_MODEL, IN_WIDTH)) * D_MODEL ** -0.5
    n_idx = jnp.arange(P, dtype=F32)
    a_re = -0.5 + 0.01 * nrm(ks[3], (DEPTH, N_DIRS, G, P))
    a_im = math.pi * n_idx + 0.01 * nrm(ks[4], (DEPTH, N_DIRS, G, P))
    log_step = jax.random.uniform(ks[5], (DEPTH, N_DIRS, G), dtype=F32,
                                  minval=math.log(DT_MIN), maxval=math.log(DT_MAX))
    b_re = nrm(ks[6], (DEPTH, N_DIRS, G, P, C)) * (2.0 * C) ** -0.5
    b_im = nrm(ks[7], (DEPTH, N_DIRS, G, P, C)) * (2.0 * C) ** -0.5
    c_re = nrm(ks[8], (DEPTH, N_DIRS, G, C, P)) * P ** -0.5
    c_im = nrm(ks[9], (DEPTH, N_DIRS, G, C, P)) * P ** -0.5
    d_skip = 0.5 * nrm(ks[10], (DEPTH, G, C))
    w_glu = nrm(ks[11], (DEPTH, SSM_WIDTH, SSM_WIDTH)) * SSM_WIDTH ** -0.5
    sink = 0.5 * nrm(ks[12], (DEPTH, N_Q_HEADS))
    norm_attn_g = 1.0 + 0.02 * nrm(ks[13], (DEPTH, ATTN_WIDTH))
    norm_ssm_g = 1.0 + 0.02 * nrm(ks[14], (DEPTH, SSM_WIDTH))
    w_out = nrm(ks[15], (DEPTH, MIX_WIDTH, D_MODEL)) * MIX_WIDTH ** -0.5
    norm_ffn_g = 1.0 + 0.02 * nrm(ks[16], (DEPTH, D_MODEL))
    w_up = nrm(ks[17], (DEPTH, D_MODEL, 2 * D_FF)) * D_MODEL ** -0.5
    conv_w = nrm(ks[18], (DEPTH, CONV_WIDTH, 2 * D_FF)) * CONV_WIDTH ** -0.5
    conv_b = 0.02 * nrm(ks[19], (DEPTH, 2 * D_FF))
    w_down = nrm(ks[20], (DEPTH, D_FF, D_MODEL)) * D_FF ** -0.5
    norm_final_g = 1.0 + 0.02 * nrm(ks[21], (D_MODEL,))
    return {"x": x, "norm_mix_g": norm_mix_g, "w_in": w_in, "a_re": a_re, "a_im": a_im,
            "log_step": log_step, "b_re": b_re, "b_im": b_im, "c_re": c_re, "c_im": c_im,
            "d_skip": d_skip, "w_glu": w_glu, "sink": sink, "norm_attn_g": norm_attn_g,
            "norm_ssm_g": norm_ssm_g, "w_out": w_out, "norm_ffn_g": norm_ffn_g, "w_up": w_up,
            "conv_w": conv_w, "conv_b": conv_b, "w_down": w_down, "norm_final_g": norm_final_g}


def reference(x, norm_mix_g, w_in, a_re, a_im, log_step, b_re, b_im, c_re, c_im, d_skip, w_glu,
              sink, norm_attn_g, norm_ssm_g, w_out, norm_ffn_g, w_up, conv_w, conv_b, w_down,
              norm_final_g):
    bsz, l = x.shape[0], x.shape[1]
    pos = jnp.arange(l)
    for i in range(DEPTH):
        h = rms_norm(x, norm_mix_g[i])
        proj = h @ w_in[i]
        q = proj[..., :ATTN_WIDTH].reshape(bsz, l, N_Q_HEADS, HEAD_DIM)
        k = proj[..., ATTN_WIDTH:ATTN_WIDTH + KV_WIDTH].reshape(bsz, l, N_KV_HEADS, HEAD_DIM)
        v = proj[..., ATTN_WIDTH + KV_WIDTH:ATTN_WIDTH + 2 * KV_WIDTH].reshape(bsz, l, N_KV_HEADS, HEAD_DIM)
        u = proj[..., ATTN_WIDTH + 2 * KV_WIDTH:]
        q = partial_rope(q, pos)
        k = partial_rope(k, pos)
        attn = window_attention(q, k, v, sink[i])
        ys = s5_bidirectional(u, a_re[i], a_im[i], log_step[i], b_re[i], b_im[i],
                              c_re[i], c_im[i], d_skip[i])
        ys = jax.nn.gelu(ys, approximate=False)
        ys = ys * jax.nn.sigmoid(ys @ w_glu[i].astype(F32))
        mixed = jnp.concatenate([rms_norm(attn, norm_attn_g[i]), rms_norm(ys, norm_ssm_g[i])], axis=-1)
        x = x + (mixed.astype(x.dtype) @ w_out[i]).astype(x.dtype)
        h = rms_norm(x, norm_ffn_g[i])
        up = depthwise_conv(h @ w_up[i], conv_w[i], conv_b[i])
        gate = up[..., :D_FF]
        val = up[..., D_FF:]
        x = x + ((jax.nn.silu(gate) * val) @ w_down[i]).astype(x.dtype)
    return rms_norm(x, norm_final_g)
```

```python
import functools
import math

import jax
import jax.numpy as jnp
from jax import lax
from jax.experimental import pallas as pl
from jax.experimental.pallas import tpu as pltpu

F32 = jnp.float32
BF16 = jnp.bfloat16

D_MODEL = 1024
BATCH = 4
SEQ = 4096
N_TOK = BATCH * SEQ
N_Q_HEADS = 8
N_KV_HEADS = 2
HEAD_DIM = 64
ATTN_WIDTH = N_Q_HEADS * HEAD_DIM
KV_WIDTH = N_KV_HEADS * HEAD_DIM
WINDOW = 128
BLOCK = 128
ROPE_DIM = HEAD_DIM // 4
ROPE_HALF = ROPE_DIM // 2
ROPE_THETA = 500000.0
SSM_WIDTH = D_MODEL // 2
SSM_GROUP = 16
N_SSM_GROUPS = SSM_WIDTH // SSM_GROUP
SSM_STATE = 64
D_FF = 2816
EPS = 1e-6

LANES = 128
SUBLANES = 8
BF16_ROWS = 16

SSM_CHUNK = 16
N_CHUNKS = SEQ // SSM_CHUNK
N_PAIRS = N_SSM_GROUPS // 2
PAIR_W = 2 * SSM_CHUNK * SSM_GROUP
SSM_ROWS = N_CHUNKS * BATCH

PROJ_W = ATTN_WIDTH + 4 * KV_WIDTH + SSM_WIDTH
TM_PROJ = 512
TM_MIX = 512
TM_FFN = 512
FF_CHUNK = 256
N_FF_CHUNKS = D_FF // FF_CHUNK
HALO = BF16_ROWS
VMEM_LIMIT = 56 * 1024 * 1024


def _rms(x, g):
    return x * lax.rsqrt(jnp.mean(x * x, axis=-1, keepdims=True) + EPS) * g


def _proj_kernel(x_ref, g_ref, w_ref, cos_ref, sa_ref, sb_ref, q_ref, kk_ref, vv_ref, u_ref):
    h = _rms(x_ref[...], g_ref[...]).astype(BF16)
    p = jnp.dot(h, w_ref[...], preferred_element_type=F32)
    c, sa, sb = cos_ref[...], sa_ref[...], sb_ref[...]

    def rope(t):
        return (t * c + pltpu.roll(t, LANES - ROPE_HALF, 1) * sa + pltpu.roll(t, ROPE_HALF, 1) * sb)

    for j in range(ATTN_WIDTH // LANES):
        q_ref[:, j * LANES:(j + 1) * LANES] = rope(p[:, j * LANES:(j + 1) * LANES]).astype(BF16)
    o = ATTN_WIDTH
    for j in range(2):
        kk_ref[:, j * LANES:(j + 1) * LANES] = rope(p[:, o + j * LANES:o + (j + 1) * LANES]).astype(BF16)
    o += 2 * KV_WIDTH
    vv_ref[...] = p[:, o:o + 2 * KV_WIDTH].astype(BF16)
    o += 2 * KV_WIDTH
    u_ref[...] = p[:, o:o + SSM_WIDTH].astype(BF16)


def _proj(x2, g, w, cos_t, sa_t, sb_t):
    tm = TM_PROJ
    n_seq_tiles = SEQ // tm
    row = lambda i: (i, 0)
    fixed = lambda i: (0, 0)
    pos = lambda i: (i % n_seq_tiles, 0)
    return pl.pallas_call(
        _proj_kernel,
        grid=(N_TOK // tm,),
        in_specs=[pl.BlockSpec((tm, D_MODEL), row), pl.BlockSpec((1, D_MODEL), fixed),
                  pl.BlockSpec((D_MODEL, PROJ_W), fixed),
                  pl.BlockSpec((tm, LANES), pos), pl.BlockSpec((tm, LANES), pos), pl.BlockSpec((tm, LANES), pos)],
        out_specs=[pl.BlockSpec((tm, ATTN_WIDTH), row), pl.BlockSpec((tm, 2 * KV_WIDTH), row),
                   pl.BlockSpec((tm, 2 * KV_WIDTH), row), pl.BlockSpec((tm, SSM_WIDTH), row)],
        out_shape=[jax.ShapeDtypeStruct((N_TOK, ATTN_WIDTH), BF16), jax.ShapeDtypeStruct((N_TOK, 2 * KV_WIDTH), BF16),
                   jax.ShapeDtypeStruct((N_TOK, 2 * KV_WIDTH), BF16), jax.ShapeDtypeStruct((N_TOK, SSM_WIDTH), BF16)],
        compiler_params=pltpu.CompilerParams(dimension_semantics=("arbitrary",), vmem_limit_bytes=VMEM_LIMIT),
        name="proj",
    )(x2, g, w, cos_t, sa_t, sb_t)


def _attn_kernel(sink_ref, q_ref, kp_ref, kc_ref, kn_ref, vp_ref, vc_ref, vn_ref, g_ref, o_ref):
    n = pl.program_id(1)
    nb = pl.num_programs(1)
    kcat = jnp.concatenate([kp_ref[0], kc_ref[0], kn_ref[0]], axis=0)
    vcat = jnp.concatenate([vp_ref[0], vc_ref[0], vn_ref[0]], axis=0)
    k_n, k_s = kcat[:, :LANES], kcat[:, LANES:]
    v_n, v_s = vcat[:, :LANES], vcat[:, LANES:]
    lo_k = lax.broadcasted_iota(jnp.int32, (3 * BLOCK, LANES), 1) < HEAD_DIM
    zero = jnp.zeros_like(k_n)
    k_even = (jnp.where(lo_k, k_n, zero), jnp.where(lo_k, k_s, zero))
    k_odd = (jnp.where(lo_k, zero, k_s), jnp.where(lo_k, zero, k_n))
    v_even = (v_n, v_s)
    v_odd = (v_s, v_n)

    qi = lax.broadcasted_iota(jnp.int32, (2 * BLOCK, 3 * BLOCK), 0) % BLOCK
    kj = lax.broadcasted_iota(jnp.int32, (2 * BLOCK, 3 * BLOCK), 1)
    rel = kj - BLOCK - qi
    valid = (jnp.abs(rel) <= WINDOW) & ((kj >= BLOCK) | (n > 0)) & ((kj < 2 * BLOCK) | (n < nb - 1))
    upper = lax.broadcasted_iota(jnp.int32, (2 * BLOCK, 1), 0) < BLOCK
    lo_o = lax.broadcasted_iota(jnp.int32, (2 * BLOCK, LANES), 1) < HEAD_DIM
    dn = (((1,), (1,)), ((), ()))

    cols = []
    for grp in range(N_KV_HEADS):
        c0 = 2 * grp
        qg = jnp.concatenate([q_ref[0, :, c0 * LANES:(c0 + 1) * LANES],
                              q_ref[0, :, (c0 + 1) * LANES:(c0 + 2) * LANES]], axis=0)
        outs = []
        for odd in range(2):
            kmat = (k_odd if odd else k_even)[grp]
            vmat = (v_odd if odd else v_even)[grp]
            s = lax.dot_general(qg, kmat, dn, preferred_element_type=F32)
            s = jnp.where(valid, s, -jnp.inf)
            h_top = 4 * grp + odd
            sink = jnp.where(upper, sink_ref[h_top], sink_ref[h_top + 2])
            m = jnp.maximum(jnp.max(s, axis=-1, keepdims=True), sink)
            p = jnp.exp(s - m)
            denom = jnp.sum(p, axis=-1, keepdims=True) + jnp.exp(sink - m)
            o = jnp.dot(p.astype(BF16), vmat, preferred_element_type=F32)
            outs.append(o / denom)
        both = jnp.where(lo_o, outs[0], outs[1])
        cols.append(both[:BLOCK])
        cols.append(both[BLOCK:])
    ms = sum(jnp.sum(c * c, axis=-1, keepdims=True) for c in cols) * (1.0 / ATTN_WIDTH)
    inv = lax.rsqrt(ms + EPS)
    for j, c in enumerate(cols):
        o_ref[0, :, j * LANES:(j + 1) * LANES] = (c * inv * g_ref[:, j * LANES:(j + 1) * LANES]).astype(BF16)


def _attn(sink, q3, kk3, vv3, g):
    nb = SEQ // BLOCK
    cur = lambda b, n: (b, n, 0)
    prev = lambda b, n: (b, jnp.maximum(n - 1, 0), 0)
    nxt = lambda b, n: (b, jnp.minimum(n + 1, nb - 1), 0)
    kv_blk = (1, BLOCK, 2 * KV_WIDTH)
    return pl.pallas_call(
        _attn_kernel,
        grid=(BATCH, nb),
        in_specs=[pl.BlockSpec(memory_space=pltpu.SMEM),
                  pl.BlockSpec((1, BLOCK, ATTN_WIDTH), cur),
                  pl.BlockSpec(kv_blk, prev), pl.BlockSpec(kv_blk, cur), pl.BlockSpec(kv_blk, nxt),
                  pl.BlockSpec(kv_blk, prev), pl.BlockSpec(kv_blk, cur), pl.BlockSpec(kv_blk, nxt),
                  pl.BlockSpec((1, ATTN_WIDTH), lambda b, n: (0, 0))],
        out_specs=pl.BlockSpec((1, BLOCK, ATTN_WIDTH), cur),
        out_shape=jax.ShapeDtypeStruct((BATCH, SEQ, ATTN_WIDTH), BF16),
        compiler_params=pltpu.CompilerParams(dimension_semantics=("arbitrary", "arbitrary"),
                                             vmem_limit_bytes=VMEM_LIMIT),
        name="attn",
    )(sink, q3, kk3, kk3, kk3, vv3, vv3, vv3, g)


def _ssm_kernel(u_ref, f_ref, w_ref, a_ref, y_ref, s_scr, xp_scr):
    u = u_ref[0]
    s_scr[...] = jnp.dot(u, f_ref[0], preferred_element_type=F32)
    a = a_ref[0]
    shape = (SUBLANES, LANES)
    af_r, af_i = jnp.broadcast_to(a[0:1], shape), jnp.broadcast_to(a[1:2], shape)
    ab_r, ab_i = jnp.broadcast_to(a[2:3], shape), jnp.broadcast_to(a[3:4], shape)
    low = lax.broadcasted_iota(jnp.int32, shape, 0) < BATCH
    n_steps = SSM_ROWS // SUBLANES

    def two_chunks(xr, xi, sr, si, ar, ai, first_low):
        cr, ci = pltpu.roll(xr, BATCH, 0), pltpu.roll(xi, BATCH, 0)
        n1r = ar * cr - ai * ci + sr
        n1i = ar * ci + ai * cr + si
        tr, ti = pltpu.roll(n1r, BATCH, 0), pltpu.roll(n1i, BATCH, 0)
        n2r = ar * tr - ai * ti + sr
        n2i = ar * ti + ai * tr + si
        sel = low if first_low else jnp.logical_not(low)
        return (jnp.where(sel, n1r, n2r), jnp.where(sel, n1i, n2i),
                jnp.where(sel, cr, tr), jnp.where(sel, ci, ti))

    def body(k, carry):
        fr, fi, br, bi = carry
        rf = pl.multiple_of(k * SUBLANES, SUBLANES)
        rb = pl.multiple_of((n_steps - 1 - k) * SUBLANES, SUBLANES)
        fr, fi, pfr, pfi = two_chunks(fr, fi, s_scr[pl.ds(rf, SUBLANES), 0:LANES],
                                      s_scr[pl.ds(rf, SUBLANES), LANES:2 * LANES], af_r, af_i, True)
        xp_scr[pl.ds(rf, SUBLANES), 0:LANES] = pfr
        xp_scr[pl.ds(rf, SUBLANES), LANES:2 * LANES] = pfi
        br, bi, pbr, pbi = two_chunks(br, bi, s_scr[pl.ds(rb, SUBLANES), 2 * LANES:3 * LANES],
                                      s_scr[pl.ds(rb, SUBLANES), 3 * LANES:4 * LANES], ab_r, ab_i, False)
        xp_scr[pl.ds(rb, SUBLANES), 2 * LANES:3 * LANES] = pbr
        xp_scr[pl.ds(rb, SUBLANES), 3 * LANES:4 * LANES] = pbi
        return fr, fi, br, bi

    z = jnp.zeros(shape, F32)
    lax.fori_loop(0, n_steps, body, (z, z, z, z))
    y = jnp.dot(u, w_ref[0, 0:PAIR_W, :], preferred_element_type=F32)
    y += jnp.dot(xp_scr[...].astype(BF16), w_ref[0, PAIR_W:2 * PAIR_W, :], preferred_element_type=F32)
    y_ref[0] = y


def _ssm(u_pairs, f_pairs, w_pairs, a_tab):
    blk = lambda i: (i, 0, 0)
    return pl.pallas_call(
        _ssm_kernel,
        grid=(N_PAIRS,),
        in_specs=[pl.BlockSpec((1, SSM_ROWS, PAIR_W), blk), pl.BlockSpec((1, PAIR_W, PAIR_W), blk),
                  pl.BlockSpec((1, 2 * PAIR_W, PAIR_W), blk), pl.BlockSpec((1, SUBLANES, LANES), blk)],
        out_specs=pl.BlockSpec((1, SSM_ROWS, PAIR_W), blk),
        out_shape=jax.ShapeDtypeStruct((N_PAIRS, SSM_ROWS, PAIR_W), F32),
        scratch_shapes=[pltpu.VMEM((SSM_ROWS, PAIR_W), F32), pltpu.VMEM((SSM_ROWS, PAIR_W), F32)],
        compiler_params=pltpu.CompilerParams(dimension_semantics=("arbitrary",), vmem_limit_bytes=VMEM_LIMIT),
        name="ssm",
    )(u_pairs, f_pairs, w_pairs, a_tab)


def _ssm_matrices(a_re, a_im, log_step, b_re, b_im, c_re, c_im, d_skip):
    t, g, p, c = SSM_CHUNK, N_SSM_GROUPS, SSM_STATE, SSM_GROUP
    hi = lax.Precision.HIGHEST
    step = jnp.exp(log_step)[..., None]
    ang = a_im * step
    mag = jnp.exp(a_re * step)
    lbr, lbi = mag * jnp.cos(ang), mag * jnp.sin(ang)
    den = a_re * a_re + a_im * a_im
    qr = ((lbr - 1.0) * a_re + lbi * a_im) / den
    qi = (lbi * a_re - (lbr - 1.0) * a_im) / den
    bbr = qr[..., None] * b_re - qi[..., None] * b_im
    bbi = qr[..., None] * b_im + qi[..., None] * b_re
    j = jnp.arange(t + 1, dtype=F32)[:, None, None, None]
    pmag = jnp.exp(a_re * step * j)
    pr, pi = pmag * jnp.cos(ang * j), pmag * jnp.sin(ang * j)
    cpr = c_re[None] * pr[:, :, :, None, :] - c_im[None] * pi[:, :, :, None, :]
    cpi = c_re[None] * pi[:, :, :, None, :] + c_im[None] * pr[:, :, :, None, :]
    kern = (jnp.einsum('jdgcp,dgpk->jdgck', cpr[:t], bbr, precision=hi)
            - jnp.einsum('jdgcp,dgpk->jdgck', cpi[:t], bbi, precision=hi))
    lag = jnp.arange(t)[:, None] - jnp.arange(t)[None, :]
    mf = jnp.where((lag >= 0)[:, :, None, None, None], kern[jnp.clip(lag, 0, t - 1), 0], 0.0)
    mb = jnp.where((lag <= 0)[:, :, None, None, None], kern[jnp.clip(-lag, 0, t - 1), 1], 0.0)
    eye_t = jnp.eye(t, dtype=F32)
    eye_c = jnp.eye(c, dtype=F32)
    m = mf + mb + eye_t[:, :, None, None, None] * (d_skip[None, None, :, :, None] * eye_c[None, None, None])
    m = jnp.transpose(m, (2, 1, 4, 0, 3)).reshape(g, t * c, t * c)

    def state_in(d, powers):
        fr = powers[0][:, :, :, None] * bbr[d][None] - powers[1][:, :, :, None] * bbi[d][None]
        fi = powers[0][:, :, :, None] * bbi[d][None] + powers[1][:, :, :, None] * bbr[d][None]
        tr = lambda z: jnp.transpose(z, (1, 0, 3, 2)).reshape(g, t * c, p)
        return tr(fr), tr(fi)

    ffr, ffi = state_in(0, (pr[t - 1::-1, 0][:t], pi[t - 1::-1, 0][:t]))
    fbr, fbi = state_in(1, (pr[:t, 1], pi[:t, 1]))

    def state_out(d, idx):
        er = jnp.transpose(cpr[idx, d], (1, 3, 0, 2)).reshape(g, p, t * c)
        ei = jnp.transpose(-cpi[idx, d], (1, 3, 0, 2)).reshape(g, p, t * c)
        return er, ei

    efr, efi = state_out(0, jnp.arange(1, t + 1))
    ebr, ebi = state_out(1, jnp.arange(t, 0, -1))

    eye2 = jnp.eye(2, dtype=F32)

    def pair_diag(z):
        k, nn = z.shape[1], z.shape[2]
        z = z.reshape(N_PAIRS, 2, k, nn)
        return (z[:, :, :, None, :] * eye2[None, :, None, :, None]).reshape(N_PAIRS, 2 * k, 2 * nn)

    f_pairs = jnp.concatenate([pair_diag(ffr), pair_diag(ffi), pair_diag(fbr), pair_diag(fbi)], axis=2)
    w_pairs = jnp.concatenate([pair_diag(m), pair_diag(efr), pair_diag(efi), pair_diag(ebr), pair_diag(ebi)], axis=1)
    a_rows = jnp.stack([pr[t, 0], pi[t, 0], pr[t, 1], pi[t, 1]], axis=1)
    a_rows = jnp.transpose(a_rows.reshape(N_PAIRS, 2, 4, p), (0, 2, 1, 3)).reshape(N_PAIRS, 4, 2 * p)
    a_tab = jnp.concatenate([a_rows, jnp.zeros_like(a_rows)], axis=1)
    return f_pairs.astype(BF16), w_pairs.astype(BF16), a_tab


def _mix_kernel(x_ref, a_ref, y_ref, wglu_ref, gs_ref, wo_ref, gf_ref, x1_ref, h2_ref):
    ys = y_ref[...]
    ys = ys * (lax.erf(ys * (1.0 / math.sqrt(2.0))) + 1.0) * 0.5
    z = jnp.dot(ys.astype(BF16), wglu_ref[...], preferred_element_type=F32)
    ys = ys * (1.0 / (1.0 + jnp.exp(-z)))
    ysn = _rms(ys, gs_ref[...]).astype(BF16)
    x1 = x_ref[...] + jnp.dot(a_ref[...], wo_ref[0:ATTN_WIDTH, :], preferred_element_type=F32)
    x1 += jnp.dot(ysn, wo_ref[ATTN_WIDTH:, :], preferred_element_type=F32)
    x1_ref[...] = x1
    h2_ref[...] = _rms(x1, gf_ref[...]).astype(BF16)


def _mix(x2, attn_n, ys_raw, w_glu, g_ssm, w_out, g_ffn):
    tm = TM_MIX
    row = lambda i: (i, 0)
    fixed = lambda i: (0, 0)
    return pl.pallas_call(
        _mix_kernel,
        grid=(N_TOK // tm,),
        in_specs=[pl.BlockSpec((tm, D_MODEL), row), pl.BlockSpec((tm, ATTN_WIDTH), row),
                  pl.BlockSpec((tm, SSM_WIDTH), row), pl.BlockSpec((SSM_WIDTH, SSM_WIDTH), fixed),
                  pl.BlockSpec((1, SSM_WIDTH), fixed), pl.BlockSpec((D_MODEL, D_MODEL), fixed),
                  pl.BlockSpec((1, D_MODEL), fixed)],
        out_specs=[pl.BlockSpec((tm, D_MODEL), row), pl.BlockSpec((tm, D_MODEL), row)],
        out_shape=[jax.ShapeDtypeStruct((N_TOK, D_MODEL), F32), jax.ShapeDtypeStruct((N_TOK, D_MODEL), BF16)],
        compiler_params=pltpu.CompilerParams(dimension_semantics=("arbitrary",), vmem_limit_bytes=VMEM_LIMIT),
        name="mix",
    )(x2, attn_n, ys_raw, w_glu, g_ssm, w_out, g_ffn)


def _ffn_kernel(h_ref, hp_ref, hn_ref, x1_ref, wup_ref, cw_ref, wdn_ref, gfin_ref, o_ref, hx_scr, acc_scr):
    tm = TM_FFN
    i = pl.program_id(0)
    tiles_per_seq = SEQ // tm
    t_in_seq = i % tiles_per_seq
    keep_prev = (t_in_seq > 0).astype(BF16)
    keep_next = (t_in_seq < tiles_per_seq - 1).astype(BF16)
    hx_scr[0:HALO, :] = hp_ref[...] * keep_prev
    hx_scr[HALO:HALO + tm, :] = h_ref[...]
    hx_scr[HALO + tm:, :] = hn_ref[...] * keep_next
    acc_scr[...] = x1_ref[...]

    def body(j, carry):
        up = jnp.dot(hx_scr[...], wup_ref[j], preferred_element_type=F32)
        cw = cw_ref[j]
        conv = (up[HALO - 1:HALO - 1 + tm] * cw[0:1] + up[HALO:HALO + tm] * cw[1:2]
                + up[HALO + 1:HALO + 1 + tm] * cw[2:3] + cw[3:4])
        gate, val = conv[:, :FF_CHUNK], conv[:, FF_CHUNK:]
        act = gate * (1.0 / (1.0 + jnp.exp(-gate))) * val
        acc_scr[...] += jnp.dot(act.astype(BF16), wdn_ref[j], preferred_element_type=F32)
        return carry

    lax.fori_loop(0, N_FF_CHUNKS, body, 0)
    o_ref[...] = _rms(acc_scr[...], gfin_ref[...])


def _ffn(h2, x1, wup3, cw3, wdn3, g_fin):
    tm = TM_FFN
    hb = tm // HALO
    n_halo_blocks = N_TOK // HALO
    row = lambda i: (i, 0)
    fixed2 = lambda i: (0, 0)
    fixed3 = lambda i: (0, 0, 0)
    prev = lambda i: (jnp.maximum(i * hb - 1, 0), 0)
    nxt = lambda i: (jnp.minimum((i + 1) * hb, n_halo_blocks - 1), 0)
    return pl.pallas_call(
        _ffn_kernel,
        grid=(N_TOK // tm,),
        in_specs=[pl.BlockSpec((tm, D_MODEL), row), pl.BlockSpec((HALO, D_MODEL), prev),
                  pl.BlockSpec((HALO, D_MODEL), nxt), pl.BlockSpec((tm, D_MODEL), row),
                  pl.BlockSpec((N_FF_CHUNKS, D_MODEL, 2 * FF_CHUNK), fixed3),
                  pl.BlockSpec((N_FF_CHUNKS, SUBLANES, 2 * FF_CHUNK), fixed3),
                  pl.BlockSpec((N_FF_CHUNKS, FF_CHUNK, D_MODEL), fixed3),
                  pl.BlockSpec((1, D_MODEL), fixed2)],
        out_specs=pl.BlockSpec((tm, D_MODEL), row),
        out_shape=jax.ShapeDtypeStruct((N_TOK, D_MODEL), F32),
        scratch_shapes=[pltpu.VMEM((tm + 2 * HALO, D_MODEL), BF16), pltpu.VMEM((tm, D_MODEL), F32)],
        compiler_params=pltpu.CompilerParams(dimension_semantics=("arbitrary",), vmem_limit_bytes=VMEM_LIMIT),
        name="ffn",
    )(h2, h2, h2, x1, wup3, cw3, wdn3, g_fin)


def _rope_tables():
    inv_freq = jnp.power(ROPE_THETA, -jnp.arange(ROPE_HALF, dtype=F32) / ROPE_HALF)
    ang = jnp.arange(SEQ, dtype=F32)[:, None] * inv_freq[None, :]
    cos, sin = jnp.cos(ang), jnp.sin(ang)
    pad = jnp.zeros((SEQ, HEAD_DIM - ROPE_DIM), F32)
    zero = jnp.zeros_like(sin)
    c_head = jnp.concatenate([cos, cos, pad + 1.0], axis=1)
    a_head = jnp.concatenate([-sin, zero, pad], axis=1)
    b_head = jnp.concatenate([zero, sin, pad], axis=1)
    two = lambda z: jnp.concatenate([z, z], axis=1)
    return two(c_head), two(a_head), two(b_head)


def kernel(x, norm_mix_g, w_in, a_re, a_im, log_step, b_re, b_im, c_re, c_im, d_skip, w_glu, sink,
           norm_attn_g, norm_ssm_g, w_out, norm_ffn_g, w_up, conv_w, conv_b, w_down, norm_final_g):
    assert x.shape == (BATCH, SEQ, D_MODEL) and w_in.shape[0] == 1, "single-layer block of the stated shapes"
    x2 = x.reshape(N_TOK, D_MODEL)
    cos_t, sa_t, sb_t = _rope_tables()
    wq = w_in[0][:, :ATTN_WIDTH] * (HEAD_DIM ** -0.5)
    wk = w_in[0][:, ATTN_WIDTH:ATTN_WIDTH + KV_WIDTH]
    wv = w_in[0][:, ATTN_WIDTH + KV_WIDTH:ATTN_WIDTH + 2 * KV_WIDTH]
    wu = w_in[0][:, ATTN_WIDTH + 2 * KV_WIDTH:]
    swap = lambda w: jnp.concatenate([w[:, HEAD_DIM:], w[:, :HEAD_DIM]], axis=1)
    w_proj = jnp.concatenate([wq, wk, swap(wk), wv, swap(wv), wu], axis=1).astype(BF16)
    q, kk, vv, u = _proj(x2, norm_mix_g[0][None], w_proj, cos_t, sa_t, sb_t)

    attn_n = _attn(sink[0], q.reshape(BATCH, SEQ, ATTN_WIDTH), kk.reshape(BATCH, SEQ, 2 * KV_WIDTH),
                   vv.reshape(BATCH, SEQ, 2 * KV_WIDTH), norm_attn_g[0][None])
    attn_n = attn_n.reshape(N_TOK, ATTN_WIDTH)

    f_pairs, w_pairs, a_tab = _ssm_matrices(a_re[0], a_im[0], log_step[0], b_re[0], b_im[0],
                                            c_re[0], c_im[0], d_skip[0])
    u6 = u.reshape(BATCH, N_CHUNKS, SSM_CHUNK, N_PAIRS, 2, SSM_GROUP)
    u_pairs = jnp.transpose(u6, (3, 1, 0, 4, 2, 5)).reshape(N_PAIRS, SSM_ROWS, PAIR_W)
    y_pairs = _ssm(u_pairs, f_pairs, w_pairs, a_tab)
    y6 = y_pairs.reshape(N_PAIRS, N_CHUNKS, BATCH, 2, SSM_CHUNK, SSM_GROUP)
    ys_raw = jnp.transpose(y6, (2, 1, 4, 0, 3, 5)).reshape(N_TOK, SSM_WIDTH)

    x1, h2 = _mix(x2, attn_n, ys_raw, w_glu[0].astype(BF16), norm_ssm_g[0][None], w_out[0].astype(BF16),
                  norm_ffn_g[0][None])

    wup3 = jnp.concatenate([w_up[0][:, :D_FF].reshape(D_MODEL, N_FF_CHUNKS, FF_CHUNK),
                            w_up[0][:, D_FF:].reshape(D_MODEL, N_FF_CHUNKS, FF_CHUNK)], axis=2)
    wup3 = jnp.transpose(wup3, (1, 0, 2)).astype(BF16)
    taps = jnp.concatenate([conv_w[0], conv_b[0][None], jnp.zeros((SUBLANES - 4, 2 * D_FF), F32)], axis=0)
    cw3 = jnp.concatenate([taps[:, :D_FF].reshape(SUBLANES, N_FF_CHUNKS, FF_CHUNK),
                           taps[:, D_FF:].reshape(SUBLANES, N_FF_CHUNKS, FF_CHUNK)], axis=2)
    cw3 = jnp.transpose(cw3, (1, 0, 2))
    wdn3 = w_down[0].reshape(N_FF_CHUNKS, FF_CHUNK, D_MODEL).astype(BF16)
    out = _ffn(h2, x1, wup3, cw3, wdn3, norm_final_g[None])
    return out.reshape(BATCH, SEQ, D_MODEL)
```

```python
import math

import jax
import jax.numpy as jnp
from jax import lax
from jax.experimental import pallas as pl
from jax.experimental.pallas import tpu as pltpu

F32 = jnp.float32
BF16 = jnp.bfloat16

D_MODEL = 1024
BATCH = 4
SEQ = 4096
N_TOK = BATCH * SEQ
N_Q_HEADS = 8
N_KV_HEADS = 2
HEAD_DIM = 64
ATTN_WIDTH = N_Q_HEADS * HEAD_DIM
KV_WIDTH = N_KV_HEADS * HEAD_DIM
WINDOW = 128
BLOCK = 128
ROPE_DIM = HEAD_DIM // 4
ROPE_HALF = ROPE_DIM // 2
ROPE_THETA = 500000.0
SSM_WIDTH = D_MODEL // 2
SSM_GROUP = 16
N_SSM_GROUPS = SSM_WIDTH // SSM_GROUP
SSM_STATE = 64
D_FF = 2816
EPS = 1e-6

LANES = 128
SUBLANES = 8
BF16_ROWS = 16

SSM_CHUNK = 16
N_CHUNKS = SEQ // SSM_CHUNK
N_PAIRS = N_SSM_GROUPS // 2
GROUP_W = SSM_CHUNK * SSM_GROUP
PAIR_W = 2 * GROUP_W
STATE_W = 2 * SSM_STATE
SSM_ROWS = N_CHUNKS * BATCH
GROUPS_PER_STEP = LANES // SSM_GROUP
PAIRS_PER_STEP = GROUPS_PER_STEP // 2

PROJ_W = ATTN_WIDTH + 4 * KV_WIDTH + SSM_WIDTH
TM_PROJ = 512
TM_MIX = 512
TM_FFN = 512
FF_CHUNK = 256
N_FF_CHUNKS = D_FF // FF_CHUNK
HALO = BF16_ROWS
VMEM_LIMIT = 56 * 1024 * 1024
NT_DIMS = (((1,), (1,)), ((), ()))


def _rms(x, g):
    return x * lax.rsqrt(jnp.mean(x * x, axis=-1, keepdims=True) + EPS) * g


def _proj_kernel(x_ref, g_ref, w_ref, cos_ref, sa_ref, sb_ref, q_ref, kk_ref, vv_ref, u_ref):
    h = _rms(x_ref[...], g_ref[...]).astype(BF16)
    p = jnp.dot(h, w_ref[...], preferred_element_type=F32)
    c, sa, sb = cos_ref[...], sa_ref[...], sb_ref[...]

    def rope(t):
        return (t * c + pltpu.roll(t, LANES - ROPE_HALF, 1) * sa + pltpu.roll(t, ROPE_HALF, 1) * sb)

    for j in range(ATTN_WIDTH // LANES):
        q_ref[:, j * LANES:(j + 1) * LANES] = rope(p[:, j * LANES:(j + 1) * LANES]).astype(BF16)
    o = ATTN_WIDTH
    for j in range(2):
        kk_ref[:, j * LANES:(j + 1) * LANES] = rope(p[:, o + j * LANES:o + (j + 1) * LANES]).astype(BF16)
    o += 2 * KV_WIDTH
    vv_ref[...] = p[:, o:o + 2 * KV_WIDTH].astype(BF16)
    o += 2 * KV_WIDTH
    for c in range(TM_PROJ // SSM_CHUNK):
        u_ref[c] = p[c * SSM_CHUNK:(c + 1) * SSM_CHUNK, o:o + SSM_WIDTH]


def _proj(x2, g, w, cos_t, sa_t, sb_t):
    tm = TM_PROJ
    n_seq_tiles = SEQ // tm
    row = lambda i: (i, 0)
    fixed = lambda i: (0, 0)
    pos = lambda i: (i % n_seq_tiles, 0)
    u_row = lambda i: (i % n_seq_tiles, i // n_seq_tiles, 0, 0)
    return pl.pallas_call(
        _proj_kernel,
        grid=(N_TOK // tm,),
        in_specs=[pl.BlockSpec((tm, D_MODEL), row), pl.BlockSpec((1, D_MODEL), fixed),
                  pl.BlockSpec((D_MODEL, PROJ_W), fixed),
                  pl.BlockSpec((tm, LANES), pos), pl.BlockSpec((tm, LANES), pos), pl.BlockSpec((tm, LANES), pos)],
        out_specs=[pl.BlockSpec((tm, ATTN_WIDTH), row), pl.BlockSpec((tm, 2 * KV_WIDTH), row),
                   pl.BlockSpec((tm, 2 * KV_WIDTH), row),
                   pl.BlockSpec((tm // SSM_CHUNK, None, SSM_CHUNK, SSM_WIDTH), u_row)],
        out_shape=[jax.ShapeDtypeStruct((N_TOK, ATTN_WIDTH), BF16), jax.ShapeDtypeStruct((N_TOK, 2 * KV_WIDTH), BF16),
                   jax.ShapeDtypeStruct((N_TOK, 2 * KV_WIDTH), BF16),
                   jax.ShapeDtypeStruct((N_CHUNKS, BATCH, SSM_CHUNK, SSM_WIDTH), F32)],
        compiler_params=pltpu.CompilerParams(dimension_semantics=("arbitrary",), vmem_limit_bytes=VMEM_LIMIT),
        name="proj",
    )(x2, g, w, cos_t, sa_t, sb_t)


def _attn_kernel(sink_ref, q_ref, kp_ref, kc_ref, kn_ref, vp_ref, vc_ref, vn_ref, g_ref, o_ref):
    n = pl.program_id(1)
    nb = pl.num_programs(1)
    kcat = jnp.concatenate([kp_ref[0], kc_ref[0], kn_ref[0]], axis=0)
    vcat = jnp.concatenate([vp_ref[0], vc_ref[0], vn_ref[0]], axis=0)
    k_n, k_s = kcat[:, :LANES], kcat[:, LANES:]
    v_n, v_s = vcat[:, :LANES], vcat[:, LANES:]
    lo_k = lax.broadcasted_iota(jnp.int32, (3 * BLOCK, LANES), 1) < HEAD_DIM
    zero = jnp.zeros_like(k_n)
    k_even = (jnp.where(lo_k, k_n, zero), jnp.where(lo_k, k_s, zero))
    k_odd = (jnp.where(lo_k, zero, k_s), jnp.where(lo_k, zero, k_n))
    v_even = (v_n, v_s)
    v_odd = (v_s, v_n)

    qi = lax.broadcasted_iota(jnp.int32, (2 * BLOCK, 3 * BLOCK), 0) % BLOCK
    kj = lax.broadcasted_iota(jnp.int32, (2 * BLOCK, 3 * BLOCK), 1)
    rel = kj - BLOCK - qi
    valid = (jnp.abs(rel) <= WINDOW) & ((kj >= BLOCK) | (n > 0)) & ((kj < 2 * BLOCK) | (n < nb - 1))
    upper = lax.broadcasted_iota(jnp.int32, (2 * BLOCK, 1), 0) < BLOCK
    lo_o = lax.broadcasted_iota(jnp.int32, (2 * BLOCK, LANES), 1) < HEAD_DIM

    cols = []
    for grp in range(N_KV_HEADS):
        c0 = 2 * grp
        qg = jnp.concatenate([q_ref[0, :, c0 * LANES:(c0 + 1) * LANES],
                              q_ref[0, :, (c0 + 1) * LANES:(c0 + 2) * LANES]], axis=0)
        outs = []
        for odd in range(2):
            kmat = (k_odd if odd else k_even)[grp]
            vmat = (v_odd if odd else v_even)[grp]
            s = lax.dot_general(qg, kmat, NT_DIMS, preferred_element_type=F32)
            s = jnp.where(valid, s, -jnp.inf)
            h_top = 4 * grp + odd
            sink = jnp.where(upper, sink_ref[h_top], sink_ref[h_top + 2])
            m = jnp.maximum(jnp.max(s, axis=-1, keepdims=True), sink)
            p = jnp.exp(s - m)
            denom = jnp.sum(p, axis=-1, keepdims=True) + jnp.exp(sink - m)
            o = jnp.dot(p.astype(BF16), vmat, preferred_element_type=F32)
            outs.append(o / denom)
        both = jnp.where(lo_o, outs[0], outs[1])
        cols.append(both[:BLOCK])
        cols.append(both[BLOCK:])
    ms = sum(jnp.sum(c * c, axis=-1, keepdims=True) for c in cols) * (1.0 / ATTN_WIDTH)
    inv = lax.rsqrt(ms + EPS)
    for j, c in enumerate(cols):
        o_ref[0, :, j * LANES:(j + 1) * LANES] = (c * inv * g_ref[:, j * LANES:(j + 1) * LANES]).astype(BF16)


def _attn(sink, q3, kk3, vv3, g):
    nb = SEQ // BLOCK
    cur = lambda b, n: (b, n, 0)
    prev = lambda b, n: (b, jnp.maximum(n - 1, 0), 0)
    nxt = lambda b, n: (b, jnp.minimum(n + 1, nb - 1), 0)
    kv_blk = (1, BLOCK, 2 * KV_WIDTH)
    return pl.pallas_call(
        _attn_kernel,
        grid=(BATCH, nb),
        in_specs=[pl.BlockSpec(memory_space=pltpu.SMEM),
                  pl.BlockSpec((1, BLOCK, ATTN_WIDTH), cur),
                  pl.BlockSpec(kv_blk, prev), pl.BlockSpec(kv_blk, cur), pl.BlockSpec(kv_blk, nxt),
                  pl.BlockSpec(kv_blk, prev), pl.BlockSpec(kv_blk, cur), pl.BlockSpec(kv_blk, nxt),
                  pl.BlockSpec((1, ATTN_WIDTH), lambda b, n: (0, 0))],
        out_specs=pl.BlockSpec((1, BLOCK, ATTN_WIDTH), cur),
        out_shape=jax.ShapeDtypeStruct((BATCH, SEQ, ATTN_WIDTH), BF16),
        compiler_params=pltpu.CompilerParams(dimension_semantics=("arbitrary", "arbitrary"),
                                             vmem_limit_bytes=VMEM_LIMIT),
        name="attn",
    )(sink, q3, kk3, kk3, kk3, vv3, vv3, vv3, g)


def _ssm_prep_kernel(par_ref, c_ref, bt_ref, dw_ref, ft_ref, wm_ref, we_ref, at_ref):
    t, cw = SSM_CHUNK, SSM_GROUP
    hi = lax.broadcasted_iota(jnp.int32, (1, LANES), 1) >= SSM_STATE
    power = lax.broadcasted_iota(jnp.int32, (3 * SUBLANES, LANES), 0).astype(F32)
    tok_of_lane = lax.broadcasted_iota(jnp.int32, (cw, GROUP_W), 1) // cw
    same_chan = (lax.broadcasted_iota(jnp.int32, (cw, GROUP_W), 1) % cw
                 == lax.broadcasted_iota(jnp.int32, (cw, GROUP_W), 0))
    ft_ref[...] = jnp.zeros(ft_ref.shape, BF16)
    wm_ref[...] = jnp.zeros(wm_ref.shape, BF16)
    we_ref[...] = jnp.zeros(we_ref.shape, BF16)
    decay = []
    for g2 in range(2):
        kern = []
        for d in range(2):
            a_re, a_im, log_step = par_ref[d, g2, 0:1, :], par_ref[d, g2, 1:2, :], par_ref[d, g2, 2:3, :]
            step = jnp.exp(log_step)
            e, ang = a_re * step, a_im * step
            mag = jnp.exp(e)
            lbr, lbi = mag * jnp.cos(ang), mag * jnp.sin(ang)
            den = a_re * a_re + a_im * a_im
            qr = ((lbr - 1.0) * a_re + lbi * a_im) / den
            qi = (lbi * a_re - (lbr - 1.0) * a_im) / den
            cre, cim = c_ref[0, d, g2], c_ref[1, d, g2]
            btr, bti = bt_ref[0, d, g2], bt_ref[1, d, g2]
            bbr, bbi = qr * btr - qi * bti, qr * bti + qi * btr
            bb_a, bb_b = jnp.where(hi, bbi, bbr), jnp.where(hi, bbr, -bbi)
            c_a, c_b = jnp.where(hi, -cim, cre), jnp.where(hi, -cre, -cim)
            pmag = jnp.exp(e * power)
            pr, pi = pmag * jnp.cos(ang * power), pmag * jnp.sin(ang * power)
            row = lambda z, k: jnp.broadcast_to(z[k:k + 1], (cw, LANES))
            cl = jnp.concatenate([row(pr, j) * c_a + row(pi, j) * c_b for j in range(t + 1)], axis=0)
            rhs = jnp.concatenate([bb_a] * t, axis=0)
            kern.append(lax.dot_general(cl[:GROUP_W], rhs, NT_DIMS, precision=lax.Precision.HIGHEST,
                                        preferred_element_type=F32))
            steps = [t - 1 - i for i in range(t)] if d == 0 else list(range(t))
            f = jnp.concatenate([row(pr, k) * bb_a + row(pi, k) * bb_b for k in steps], axis=0)
            f_t = f.T
            r0 = 2 * d * STATE_W + g2 * SSM_STATE
            ft_ref[0, r0:r0 + SSM_STATE, g2 * GROUP_W:(g2 + 1) * GROUP_W] = f_t[:SSM_STATE].astype(BF16)
            ft_ref[0, r0 + STATE_W:r0 + STATE_W + SSM_STATE, g2 * GROUP_W:(g2 + 1) * GROUP_W] = (
                f_t[SSM_STATE:].astype(BF16))
            if d == 0:
                e_out = cl[cw:]
            else:
                e_out = jnp.concatenate([cl[(t - i) * cw:(t - i + 1) * cw] for i in range(t)], axis=0)
            e_sw = pltpu.roll(e_out, SSM_STATE, 1)
            zero = jnp.zeros_like(e_out)
            re_blk = jnp.where(hi, zero, e_out) if g2 == 0 else jnp.where(hi, e_sw, zero)
            im_blk = jnp.where(hi, zero, e_sw) if g2 == 0 else jnp.where(hi, e_out, zero)
            rows = slice(g2 * GROUP_W, (g2 + 1) * GROUP_W)
            we_ref[0, rows, 2 * d * STATE_W:(2 * d + 1) * STATE_W] = re_blk.astype(BF16)
            we_ref[0, rows, (2 * d + 1) * STATE_W:(2 * d + 2) * STATE_W] = im_blk.astype(BF16)
            decay.append((pr[t:t + 1], pi[t:t + 1]))
        d_row = jnp.broadcast_to(dw_ref[g2, 0:1, :], (cw, GROUP_W))
        for ip in range(t):
            fwd = jnp.zeros((cw, GROUP_W), F32)
            bwd = jnp.zeros((cw, GROUP_W), F32)
            for j in range(ip + 1):
                fwd = jnp.where(tok_of_lane == ip - j, kern[0][j * cw:(j + 1) * cw], fwd)
            for j in range(t - ip):
                bwd = jnp.where(tok_of_lane == ip + j, kern[1][j * cw:(j + 1) * cw], bwd)
            skip = jnp.where((tok_of_lane == ip) & same_chan, d_row, 0.0)
            wm_ref[0, g2 * GROUP_W + ip * cw:g2 * GROUP_W + (ip + 1) * cw, g2 * GROUP_W:(g2 + 1) * GROUP_W] = (
                (fwd + bwd + skip).astype(BF16))
    a_rows = [jnp.where(hi, decay[2 + d][k], decay[d][k]) for d in range(2) for k in range(2)]
    at_ref[0] = jnp.concatenate(a_rows + [jnp.zeros((SUBLANES - 4, LANES), F32)], axis=0)


def _ssm_prep(par, cmat, bt, dwide):
    return pl.pallas_call(
        _ssm_prep_kernel,
        grid=(N_PAIRS,),
        in_specs=[pl.BlockSpec((2, 2, SUBLANES, LANES), lambda p: (0, p, 0, 0)),
                  pl.BlockSpec((2, 2, 2, SSM_GROUP, LANES), lambda p: (0, 0, p, 0, 0)),
                  pl.BlockSpec((2, 2, 2, SSM_GROUP, LANES), lambda p: (0, 0, p, 0, 0)),
                  pl.BlockSpec((2, SUBLANES, GROUP_W), lambda p: (p, 0, 0))],
        out_specs=[pl.BlockSpec((1, PAIR_W, PAIR_W), lambda p: (p, 0, 0))] * 3
        + [pl.BlockSpec((1, SUBLANES, LANES), lambda p: (p, 0, 0))],
        out_shape=[jax.ShapeDtypeStruct((N_PAIRS, PAIR_W, PAIR_W), BF16)] * 3
        + [jax.ShapeDtypeStruct((N_PAIRS, SUBLANES, LANES), F32)],
        compiler_params=pltpu.CompilerParams(dimension_semantics=("arbitrary",), vmem_limit_bytes=VMEM_LIMIT),
        name="ssm_prep",
    )(par, cmat, bt, dwide)


def _ssm_kernel(u_ref, ft_ref, wm_ref, we_ref, a_ref, y_ref, x_scr, yt_scr, st_scr, s_scr, xp_scr):
    for i in range(SSM_CHUNK):
        v_t = u_ref[pl.ds(i, SSM_ROWS, stride=SSM_CHUNK), :].T
        for g in range(GROUPS_PER_STEP):
            r0 = (g % 2) * GROUP_W + i * SSM_GROUP
            x_scr[g // 2, r0:r0 + SSM_GROUP, :] = v_t[g * SSM_GROUP:(g + 1) * SSM_GROUP, :].astype(BF16)

    shape = (SUBLANES, LANES)
    low = lax.broadcasted_iota(jnp.int32, shape, 0) < BATCH
    n_steps = SSM_ROWS // SUBLANES

    def two_chunks(xr, xi, sr, si, ar, ai, first_low):
        cr, ci = pltpu.roll(xr, BATCH, 0), pltpu.roll(xi, BATCH, 0)
        n1r = ar * cr - ai * ci + sr
        n1i = ar * ci + ai * cr + si
        tr, ti = pltpu.roll(n1r, BATCH, 0), pltpu.roll(n1i, BATCH, 0)
        n2r = ar * tr - ai * ti + sr
        n2i = ar * ti + ai * tr + si
        sel = low if first_low else jnp.logical_not(low)
        return (jnp.where(sel, n1r, n2r), jnp.where(sel, n1i, n2i),
                jnp.where(sel, cr, tr), jnp.where(sel, ci, ti))

    for p in range(PAIRS_PER_STEP):
        x = x_scr[p]
        st_scr[...] = jnp.dot(ft_ref[p], x, preferred_element_type=F32)
        s_scr[...] = st_scr[...].T
        a = a_ref[p]
        af_r, af_i = jnp.broadcast_to(a[0:1], shape), jnp.broadcast_to(a[1:2], shape)
        ab_r, ab_i = jnp.broadcast_to(a[2:3], shape), jnp.broadcast_to(a[3:4], shape)

        def body(k, carry):
            fr, fi, br, bi = carry
            rf = pl.multiple_of(k * SUBLANES, SUBLANES)
            rb = pl.multiple_of((n_steps - 1 - k) * SUBLANES, SUBLANES)
            fr, fi, pfr, pfi = two_chunks(fr, fi, s_scr[pl.ds(rf, SUBLANES), 0:LANES],
                                          s_scr[pl.ds(rf, SUBLANES), LANES:2 * LANES], af_r, af_i, True)
            xp_scr[pl.ds(rf, SUBLANES), 0:LANES] = pfr
            xp_scr[pl.ds(rf, SUBLANES), LANES:2 * LANES] = pfi
            br, bi, pbr, pbi = two_chunks(br, bi, s_scr[pl.ds(rb, SUBLANES), 2 * LANES:3 * LANES],
                                          s_scr[pl.ds(rb, SUBLANES), 3 * LANES:4 * LANES], ab_r, ab_i, False)
            xp_scr[pl.ds(rb, SUBLANES), 2 * LANES:3 * LANES] = pbr
            xp_scr[pl.ds(rb, SUBLANES), 3 * LANES:4 * LANES] = pbi
            return fr, fi, br, bi

        z = jnp.zeros(shape, F32)
        lax.fori_loop(0, n_steps, body, (z, z, z, z))
        y_t = jnp.dot(wm_ref[p], x, preferred_element_type=F32)
        y_t += lax.dot_general(we_ref[p], xp_scr[...].astype(BF16), NT_DIMS, preferred_element_type=F32)
        for g2 in range(2):
            g = 2 * p + g2
            for i in range(SSM_CHUNK):
                r0 = g2 * GROUP_W + i * SSM_GROUP
                yt_scr[i, g * SSM_GROUP:(g + 1) * SSM_GROUP, :] = y_t[r0:r0 + SSM_GROUP, :]
    for i in range(SSM_CHUNK):
        y_ref[pl.ds(i, SSM_ROWS, stride=SSM_CHUNK), :] = yt_scr[i].T


def _ssm(u2, ft, wm, we, at):
    slab = lambda i: (0, i)
    ops = lambda i: (i, 0, 0)
    once = pl.Buffered(1)
    return pl.pallas_call(
        _ssm_kernel,
        grid=(SSM_WIDTH // LANES,),
        in_specs=[pl.BlockSpec((N_TOK, LANES), slab, pipeline_mode=once),
                  pl.BlockSpec((PAIRS_PER_STEP, PAIR_W, PAIR_W), ops, pipeline_mode=once),
                  pl.BlockSpec((PAIRS_PER_STEP, PAIR_W, PAIR_W), ops, pipeline_mode=once),
                  pl.BlockSpec((PAIRS_PER_STEP, PAIR_W, PAIR_W), ops, pipeline_mode=once),
                  pl.BlockSpec((PAIRS_PER_STEP, SUBLANES, LANES), ops)],
        out_specs=pl.BlockSpec((N_TOK, LANES), slab, pipeline_mode=once),
        out_shape=jax.ShapeDtypeStruct((N_TOK, SSM_WIDTH), F32),
        scratch_shapes=[pltpu.VMEM((PAIRS_PER_STEP, PAIR_W, SSM_ROWS), BF16),
                        pltpu.VMEM((SSM_CHUNK, LANES, SSM_ROWS), F32),
                        pltpu.VMEM((4 * STATE_W, SSM_ROWS), F32),
                        pltpu.VMEM((SSM_ROWS, 4 * STATE_W), F32), pltpu.VMEM((SSM_ROWS, 4 * STATE_W), F32)],
        compiler_params=pltpu.CompilerParams(dimension_semantics=("arbitrary",), vmem_limit_bytes=VMEM_LIMIT),
        name="ssm",
    )(u2, ft, wm, we, at)


def _mix_kernel(x_ref, a_ref, y_ref, wglu_ref, gs_ref, wo_ref, gf_ref, x1_ref, h2_ref):
    ys = jnp.concatenate([y_ref[c] for c in range(TM_MIX // SSM_CHUNK)], axis=0)
    ys = ys * (lax.erf(ys * (1.0 / math.sqrt(2.0))) + 1.0) * 0.5
    z = jnp.dot(ys.astype(BF16), wglu_ref[...], preferred_element_type=F32)
    ys = ys * (1.0 / (1.0 + jnp.exp(-z)))
    ysn = _rms(ys, gs_ref[...]).astype(BF16)
    x1 = x_ref[...] + jnp.dot(a_ref[...], wo_ref[0:ATTN_WIDTH, :], preferred_element_type=F32)
    x1 += jnp.dot(ysn, wo_ref[ATTN_WIDTH:, :], preferred_element_type=F32)
    x1_ref[...] = x1
    h2_ref[...] = _rms(x1, gf_ref[...]).astype(BF16)


def _mix(x2, attn_n, ys4, w_glu, g_ssm, w_out, g_ffn):
    tm = TM_MIX
    n_seq_tiles = SEQ // tm
    row = lambda i: (i, 0)
    fixed = lambda i: (0, 0)
    y_row = lambda i: (i % n_seq_tiles, i // n_seq_tiles, 0, 0)
    return pl.pallas_call(
        _mix_kernel,
        grid=(N_TOK // tm,),
        in_specs=[pl.BlockSpec((tm, D_MODEL), row), pl.BlockSpec((tm, ATTN_WIDTH), row),
                  pl.BlockSpec((tm // SSM_CHUNK, None, SSM_CHUNK, SSM_WIDTH), y_row),
                  pl.BlockSpec((SSM_WIDTH, SSM_WIDTH), fixed),
                  pl.BlockSpec((1, SSM_WIDTH), fixed), pl.BlockSpec((D_MODEL, D_MODEL), fixed),
                  pl.BlockSpec((1, D_MODEL), fixed)],
        out_specs=[pl.BlockSpec((tm, D_MODEL), row), pl.BlockSpec((tm, D_MODEL), row)],
        out_shape=[jax.ShapeDtypeStruct((N_TOK, D_MODEL), F32), jax.ShapeDtypeStruct((N_TOK, D_MODEL), BF16)],
        compiler_params=pltpu.CompilerParams(dimension_semantics=("arbitrary",), vmem_limit_bytes=VMEM_LIMIT),
        name="mix",
    )(x2, attn_n, ys4, w_glu, g_ssm, w_out, g_ffn)


def _ffn_kernel(h_ref, hp_ref, hn_ref, x1_ref, wup_ref, cw_ref, wdn_ref, gfin_ref, o_ref, hx_scr, acc_scr):
    tm = TM_FFN
    i = pl.program_id(0)
    tiles_per_seq = SEQ // tm
    t_in_seq = i % tiles_per_seq
    keep_prev = (t_in_seq > 0).astype(BF16)
    keep_next = (t_in_seq < tiles_per_seq - 1).astype(BF16)
    hx_scr[0:HALO, :] = hp_ref[...] * keep_prev
    hx_scr[HALO:HALO + tm, :] = h_ref[...]
    hx_scr[HALO + tm:, :] = hn_ref[...] * keep_next
    acc_scr[...] = x1_ref[...]
    hx = hx_scr[...]

    def conv_up(off):
        cols = slice(off, off + FF_CHUNK)
        up = jnp.dot(hx, wup_ref[:, cols], preferred_element_type=F32)
        return (up[HALO - 1:HALO - 1 + tm] * cw_ref[0:1, cols] + up[HALO:HALO + tm] * cw_ref[1:2, cols]
                + up[HALO + 1:HALO + 1 + tm] * cw_ref[2:3, cols] + cw_ref[3:4, cols])

    for j in range(N_FF_CHUNKS):
        gate = conv_up(j * FF_CHUNK)
        val = conv_up(D_FF + j * FF_CHUNK)
        act = gate * (1.0 / (1.0 + jnp.exp(-gate))) * val
        acc_scr[...] += jnp.dot(act.astype(BF16), wdn_ref[j * FF_CHUNK:(j + 1) * FF_CHUNK, :],
                                preferred_element_type=F32)
    o_ref[...] = _rms(acc_scr[...], gfin_ref[...])


def _ffn(h2, x1, w_up, taps, w_down, g_fin):
    tm = TM_FFN
    hb = tm // HALO
    n_halo_blocks = N_TOK // HALO
    row = lambda i: (i, 0)
    fixed = lambda i: (0, 0)
    prev = lambda i: (jnp.maximum(i * hb - 1, 0), 0)
    nxt = lambda i: (jnp.minimum((i + 1) * hb, n_halo_blocks - 1), 0)
    once = pl.Buffered(1)
    return pl.pallas_call(
        _ffn_kernel,
        grid=(N_TOK // tm,),
        in_specs=[pl.BlockSpec((tm, D_MODEL), row), pl.BlockSpec((HALO, D_MODEL), prev),
                  pl.BlockSpec((HALO, D_MODEL), nxt), pl.BlockSpec((tm, D_MODEL), row),
                  pl.BlockSpec((D_MODEL, 2 * D_FF), fixed, pipeline_mode=once),
                  pl.BlockSpec((SUBLANES, 2 * D_FF), fixed, pipeline_mode=once),
                  pl.BlockSpec((D_FF, D_MODEL), fixed, pipeline_mode=once),
                  pl.BlockSpec((1, D_MODEL), fixed)],
        out_specs=pl.BlockSpec((tm, D_MODEL), row),
        out_shape=jax.ShapeDtypeStruct((N_TOK, D_MODEL), F32),
        scratch_shapes=[pltpu.VMEM((tm + 2 * HALO, D_MODEL), BF16), pltpu.VMEM((tm, D_MODEL), F32)],
        compiler_params=pltpu.CompilerParams(dimension_semantics=("arbitrary",), vmem_limit_bytes=VMEM_LIMIT),
        name="ffn",
    )(h2, h2, h2, x1, w_up, taps, w_down, g_fin)


def _rope_tables():
    inv_freq = jnp.power(ROPE_THETA, -jnp.arange(ROPE_HALF, dtype=F32) / ROPE_HALF)
    ang = jnp.arange(SEQ, dtype=F32)[:, None] * inv_freq[None, :]
    cos, sin = jnp.cos(ang), jnp.sin(ang)
    pad = jnp.zeros((SEQ, HEAD_DIM - ROPE_DIM), F32)
    zero = jnp.zeros_like(sin)
    c_head = jnp.concatenate([cos, cos, pad + 1.0], axis=1)
    a_head = jnp.concatenate([-sin, zero, pad], axis=1)
    b_head = jnp.concatenate([zero, sin, pad], axis=1)
    two = lambda z: jnp.concatenate([z, z], axis=1)
    return two(c_head), two(a_head), two(b_head)


def _ssm_prep_inputs(a_re, a_im, log_step, b_re, b_im, c_re, c_im, d_skip):
    dup = lambda z: jnp.concatenate([z, z], axis=-1)
    rows = [dup(a_re), dup(a_im), jnp.broadcast_to(log_step[..., None], (2, N_SSM_GROUPS, LANES))]
    par = jnp.stack(rows + [jnp.zeros_like(rows[0])] * (SUBLANES - 3), axis=2)
    cmat = dup(jnp.stack([c_re, c_im]))
    bt = dup(jnp.swapaxes(jnp.stack([b_re, b_im]), -1, -2))
    dwide = jnp.broadcast_to(jnp.tile(d_skip, (1, SSM_CHUNK))[:, None, :], (N_SSM_GROUPS, SUBLANES, GROUP_W))
    return par, cmat, bt, dwide


def kernel(x, norm_mix_g, w_in, a_re, a_im, log_step, b_re, b_im, c_re, c_im, d_skip, w_glu, sink,
           norm_attn_g, norm_ssm_g, w_out, norm_ffn_g, w_up, conv_w, conv_b, w_down, norm_final_g):
    assert x.shape == (BATCH, SEQ, D_MODEL) and w_in.shape[0] == 1, "single-layer block of the stated shapes"
    x2 = x.reshape(N_TOK, D_MODEL)
    cos_t, sa_t, sb_t = _rope_tables()
    wq = w_in[0][:, :ATTN_WIDTH] * (HEAD_DIM ** -0.5)
    wk = w_in[0][:, ATTN_WIDTH:ATTN_WIDTH + KV_WIDTH]
    wv = w_in[0][:, ATTN_WIDTH + KV_WIDTH:ATTN_WIDTH + 2 * KV_WIDTH]
    wu = w_in[0][:, ATTN_WIDTH + 2 * KV_WIDTH:]
    swap = lambda w: jnp.concatenate([w[:, HEAD_DIM:], w[:, :HEAD_DIM]], axis=1)
    w_proj = jnp.concatenate([wq, wk, swap(wk), wv, swap(wv), wu], axis=1).astype(BF16)
    q, kk, vv, u4 = _proj(x2, norm_mix_g[0][None], w_proj, cos_t, sa_t, sb_t)

    attn_n = _attn(sink[0], q.reshape(BATCH, SEQ, ATTN_WIDTH), kk.reshape(BATCH, SEQ, 2 * KV_WIDTH),
                   vv.reshape(BATCH, SEQ, 2 * KV_WIDTH), norm_attn_g[0][None])
    attn_n = attn_n.reshape(N_TOK, ATTN_WIDTH)

    ft, wm, we, at = _ssm_prep(*_ssm_prep_inputs(a_re[0], a_im[0], log_step[0], b_re[0], b_im[0],
                                                 c_re[0], c_im[0], d_skip[0]))
    ys2 = _ssm(u4.reshape(N_TOK, SSM_WIDTH), ft, wm, we, at)
    ys4 = ys2.reshape(N_CHUNKS, BATCH, SSM_CHUNK, SSM_WIDTH)

    x1, h2 = _mix(x2, attn_n, ys4, w_glu[0].astype(BF16), norm_ssm_g[0][None], w_out[0].astype(BF16),
                  norm_ffn_g[0][None])

    taps = jnp.concatenate([conv_w[0], conv_b[0][None], jnp.zeros((SUBLANES - 4, 2 * D_FF), F32)], axis=0)
    out = _ffn(h2, x1, w_up[0].astype(BF16), taps, w_down[0].astype(BF16), norm_final_g[None])
    return out.reshape(BATCH, SEQ, D_MODEL)
```
